```python
import jax, jax.numpy as jnp
from jax import lax
import numpy as np

D_MODEL = 4096
BATCH = 2
SEQ = 8192
DEPTH = 2

N_MIXERS = 2
N_A = (DEPTH + 1) // 2
N_B = DEPTH // 2
SHORT_CONV_K = 3
CONFORMER_K = 31
CONFORMER_EXP = 2
D_FF = 11008
FFN_CONV_K = 3
RMS_EPS = 1e-6
LN_EPS = 1e-5

kernel_name = "hybrid_shortconv_conformer_convffn"


def rmsnorm(x, g):
    xf = x.astype(jnp.float32)
    y = xf * lax.rsqrt(jnp.mean(xf * xf, axis=-1, keepdims=True) + RMS_EPS)
    return (y * g.astype(jnp.float32)).astype(x.dtype)


def layernorm(x, g, b):
    xf = x.astype(jnp.float32)
    mu = jnp.mean(xf, axis=-1, keepdims=True)
    var = jnp.mean(jnp.square(xf - mu), axis=-1, keepdims=True)
    y = (xf - mu) * lax.rsqrt(var + LN_EPS)
    return (y * g.astype(jnp.float32) + b.astype(jnp.float32)).astype(x.dtype)


def causal_dwconv(x, w):
    k, c = w.shape
    xp = jnp.pad(x, ((0, 0), (k - 1, 0), (0, 0)))
    return lax.conv_general_dilated(
        xp, w[:, None, :].astype(x.dtype), window_strides=(1,), padding="VALID",
        dimension_numbers=("NWC", "WIO", "NWC"), feature_group_count=c)


def short_conv_mixer(x, w_in, conv_w, w_out):
    gate_b, gate_c, h = jnp.split(jnp.einsum("bsd,de->bse", x, w_in), 3, axis=-1)
    h = causal_dwconv(gate_c * h, conv_w)
    return jnp.einsum("bsd,de->bse", gate_b * h, w_out)


def conformer_conv_mixer(x, w_pw1, b_pw1, dw_w, dw_b, ln_g, ln_b, w_pw2, b_pw2):
    h = jnp.einsum("bsd,de->bse", x, w_pw1) + b_pw1
    val, gate = jnp.split(h, 2, axis=-1)
    h = val * jax.nn.sigmoid(gate)
    h = causal_dwconv(h, dw_w) + dw_b
    h = layernorm(h, ln_g, ln_b)
    h = jax.nn.silu(h)
    return jnp.einsum("bsd,de->bse", h, w_pw2) + b_pw2


def conv_ffn(x, w_up, conv_w, w_down):
    h = causal_dwconv(jnp.einsum("bsd,df->bsf", x, w_up), conv_w)
    gate, val = jnp.split(h, 2, axis=-1)
    return jnp.einsum("bsf,fd->bsd", jax.nn.silu(gate) * val, w_down)


def setup_inputs(seed: int = 0) -> dict:
    key = jax.random.key(seed)
    ks = jax.random.split(key, 24)
    d, f = D_MODEL, D_FF
    nrm = lambda k, shape, fan_in: jax.random.normal(k, shape, jnp.float32) * (fan_in ** -0.5)
    gain = lambda k, shape: 1.0 + 0.05 * jax.random.normal(k, shape, jnp.float32)
    small = lambda k, shape: 0.01 * jax.random.normal(k, shape, jnp.float32)
    return {
        "x": jax.random.normal(ks[0], (BATCH, SEQ, d), jnp.float32),
        "g_mix_pre": gain(ks[1], (DEPTH, d)),
        "g_mix_post": gain(ks[2], (DEPTH, d)),
        "g_ffn_pre": gain(ks[3], (DEPTH, d)),
        "g_ffn_post": gain(ks[4], (DEPTH, d)),
        "a_w_in": nrm(ks[5], (N_A, d, 3 * d), d),
        "a_conv_w": nrm(ks[6], (N_A, SHORT_CONV_K, d), SHORT_CONV_K),
        "a_w_out": nrm(ks[7], (N_A, d, d), d),
        "b_w_pw1": nrm(ks[8], (N_B, d, CONFORMER_EXP * d), d),
        "b_b_pw1": small(ks[9], (N_B, CONFORMER_EXP * d)),
        "b_dw_w": nrm(ks[10], (N_B, CONFORMER_K, d), CONFORMER_K),
        "b_dw_b": small(ks[11], (N_B, d)),
        "b_ln_g": gain(ks[12], (N_B, d)),
        "b_ln_b": small(ks[13], (N_B, d)),
        "b_w_pw2": nrm(ks[14], (N_B, d, d), d),
        "b_b_pw2": small(ks[15], (N_B, d)),
        "f_w_up": nrm(ks[16], (DEPTH, d, 2 * f), d),
        "f_conv_w": nrm(ks[17], (DEPTH, FFN_CONV_K, 2 * f), FFN_CONV_K),
        "f_w_down": nrm(ks[18], (DEPTH, f, d), f),
    }


def reference(x, g_mix_pre, g_mix_post, g_ffn_pre, g_ffn_post,
              a_w_in, a_conv_w, a_w_out,
              b_w_pw1, b_b_pw1, b_dw_w, b_dw_b, b_ln_g, b_ln_b, b_w_pw2, b_b_pw2,
              f_w_up, f_conv_w, f_w_down):
    for i in range(DEPTH):
        h = rmsnorm(x, g_mix_pre[i])
        j = i // N_MIXERS
        if i % N_MIXERS == 0:
            h = short_conv_mixer(h, a_w_in[j], a_conv_w[j], a_w_out[j])
        else:
            h = conformer_conv_mixer(h, b_w_pw1[j], b_b_pw1[j], b_dw_w[j], b_dw_b[j],
                                     b_ln_g[j], b_ln_b[j], b_w_pw2[j], b_b_pw2[j])
        x = x + rmsnorm(h, g_mix_post[i])
        h = conv_ffn(rmsnorm(x, g_ffn_pre[i]), f_w_up[i], f_conv_w[i], f_w_down[i])
        x = x + rmsnorm(h, g_ffn_post[i])
    return x
```

```python
import functools

import jax
import jax.numpy as jnp
from jax import lax
from jax.experimental import pallas as pl
from jax.experimental.pallas import tpu as pltpu

RMS_EPS = 1e-6
LN_EPS = 1e-5

V7X_VMEM_BYTES = 64 * 1024 * 1024
VMEM_LIMIT_BYTES = V7X_VMEM_BYTES - 8 * 1024 * 1024
SUBLANES = 8
LANES = 128


def _pick(dim, candidates):
    for c in candidates:
        if c <= dim and dim % c == 0:
            return c
    return dim


def _params(n_axes):
    return pltpu.CompilerParams(
        dimension_semantics=("arbitrary",) * n_axes,
        vmem_limit_bytes=VMEM_LIMIT_BYTES,
    )


def _rms_scale(v):
    return lax.rsqrt(jnp.mean(v * v, axis=-1, keepdims=True) + RMS_EPS)


def _silu(v):
    return v * jax.nn.sigmoid(v)


def _rmsnorm_body(x_ref, g_ref, o_ref):
    x = x_ref[...]
    o_ref[...] = (x * _rms_scale(x) * g_ref[...]).astype(o_ref.dtype)


def _rmsnorm(x, g, tm):
    m, d = x.shape
    return pl.pallas_call(
        _rmsnorm_body,
        grid=(m // tm,),
        in_specs=[pl.BlockSpec((tm, d), lambda i: (i, 0)),
                  pl.BlockSpec((1, d), lambda i: (0, 0))],
        out_specs=pl.BlockSpec((tm, d), lambda i: (i, 0)),
        out_shape=jax.ShapeDtypeStruct((m, d), jnp.bfloat16),
        compiler_params=_params(1),
        name="rmsnorm",
    )(x, g.reshape(1, d))


def _residual_norm_body(x_ref, y_ref, gpost_ref, gnext_ref, xo_ref, no_ref):
    y = y_ref[...]
    xn = x_ref[...] + y * _rms_scale(y) * gpost_ref[...]
    xo_ref[...] = xn
    no_ref[...] = (xn * _rms_scale(xn) * gnext_ref[...]).astype(no_ref.dtype)


def _residual_body(x_ref, y_ref, gpost_ref, xo_ref):
    y = y_ref[...]
    xo_ref[...] = x_ref[...] + y * _rms_scale(y) * gpost_ref[...]


def _residual_norm(x, y, g_post, g_next, tm):
    m, d = x.shape
    row = pl.BlockSpec((tm, d), lambda i: (i, 0))
    vec = pl.BlockSpec((1, d), lambda i: (0, 0))
    if g_next is None:
        return pl.pallas_call(
            _residual_body,
            grid=(m // tm,),
            in_specs=[row, row, vec],
            out_specs=row,
            out_shape=jax.ShapeDtypeStruct((m, d), jnp.float32),
            compiler_params=_params(1),
            name="residual",
        )(x, y, g_post.reshape(1, d)), None
    return pl.pallas_call(
        _residual_norm_body,
        grid=(m // tm,),
        in_specs=[row, row, vec, vec],
        out_specs=[row, row],
        out_shape=[jax.ShapeDtypeStruct((m, d), jnp.float32),
                   jax.ShapeDtypeStruct((m, d), jnp.bfloat16)],
        compiler_params=_params(1),
        name="residual_norm",
    )(x, y, g_post.reshape(1, d), g_next.reshape(1, d))


def _ln_silu_body(h_ref, g_ref, b_ref, o_ref):
    h = h_ref[...]
    mu = jnp.mean(h, axis=-1, keepdims=True)
    c = h - mu
    var = jnp.mean(c * c, axis=-1, keepdims=True)
    y = c * lax.rsqrt(var + LN_EPS) * g_ref[...] + b_ref[...]
    o_ref[...] = _silu(y).astype(o_ref.dtype)


def _ln_silu(h, g, b, tm):
    m, d = h.shape
    row = pl.BlockSpec((tm, d), lambda i: (i, 0))
    vec = pl.BlockSpec((1, d), lambda i: (0, 0))
    return pl.pallas_call(
        _ln_silu_body,
        grid=(m // tm,),
        in_specs=[row, vec, vec],
        out_specs=row,
        out_shape=jax.ShapeDtypeStruct((m, d), jnp.bfloat16),
        compiler_params=_params(1),
        name="ln_silu",
    )(h, g.reshape(1, d), b.reshape(1, d))


def _dot(a, b):
    return jnp.dot(a, b, preferred_element_type=jnp.float32)


def _matmul_body(a_ref, w_ref, o_ref):
    o_ref[...] = _dot(a_ref[...], w_ref[...])


def _matmul_bias_body(a_ref, w_ref, b_ref, o_ref):
    o_ref[...] = _dot(a_ref[...], w_ref[...]) + b_ref[...]


def _matmul(a, w, bias, tm, tn):
    m, k = a.shape
    n = w.shape[1]
    in_specs = [pl.BlockSpec((tm, k), lambda i, j: (i, 0)),
                pl.BlockSpec((k, tn), lambda i, j: (0, j))]
    args = [a, w]
    body = _matmul_body
    if bias is not None:
        in_specs.append(pl.BlockSpec((1, tn), lambda i, j: (0, j)))
        args.append(bias.reshape(1, n))
        body = _matmul_bias_body
    return pl.pallas_call(
        body,
        grid=(m // tm, n // tn),
        in_specs=in_specs,
        out_specs=pl.BlockSpec((tm, tn), lambda i, j: (i, j)),
        out_shape=jax.ShapeDtypeStruct((m, n), jnp.float32),
        compiler_params=_params(2),
        name="matmul",
    )(*args)


def _causal_conv(u, w_ref, carry_ref, stage_ref, first_in_seq, taps, halo, row_chunk):
    tm, tn = u.shape

    @pl.when(first_in_seq)
    def _():
        stage_ref[0:halo, :] = jnp.zeros((halo, tn), jnp.float32)

    @pl.when(jnp.logical_not(first_in_seq))
    def _():
        stage_ref[0:halo, :] = carry_ref[...]

    stage_ref[halo:halo + tm, :] = u
    carry_ref[...] = stage_ref[tm:tm + halo, :]

    outs = []
    for r0 in range(0, tm, row_chunk):
        acc = None
        for k in range(taps):
            off = r0 + halo - (taps - 1) + k
            term = stage_ref[off:off + row_chunk, :] * w_ref[k:k + 1, :]
            acc = term if acc is None else acc + term
        outs.append(acc)
    return outs[0] if len(outs) == 1 else jnp.concatenate(outs, axis=0)


def _short_conv_in_body(tiles_per_seq, a_ref, wb_ref, wc_ref, wh_ref, cw_ref, o_ref,
                        carry_ref, stage_ref):
    i, j = pl.program_id(0), pl.program_id(1)
    a = a_ref[...]
    u = _dot(a, wc_ref[...]) * _dot(a, wh_ref[...])
    conv = _causal_conv(u, cw_ref, carry_ref.at[j], stage_ref, i % tiles_per_seq == 0,
                        taps=cw_ref.shape[0], halo=SUBLANES, row_chunk=u.shape[0])
    o_ref[...] = (_dot(a, wb_ref[...]) * conv).astype(o_ref.dtype)


def _short_conv_in(a, w_in, conv_w, seq, tm, tn):
    m, k = a.shape
    d = w_in.shape[1] // 3
    nj = d // tn
    wspec = lambda part: pl.BlockSpec((k, tn), lambda i, j: (0, part * nj + j))
    return pl.pallas_call(
        functools.partial(_short_conv_in_body, seq // tm),
        grid=(m // tm, nj),
        in_specs=[pl.BlockSpec((tm, k), lambda i, j: (i, 0)),
                  wspec(0), wspec(1), wspec(2),
                  pl.BlockSpec((conv_w.shape[0], tn), lambda i, j: (0, j))],
        out_specs=pl.BlockSpec((tm, tn), lambda i, j: (i, j)),
        out_shape=jax.ShapeDtypeStruct((m, d), jnp.bfloat16),
        scratch_shapes=[pltpu.VMEM((nj, SUBLANES, tn), jnp.float32),
                        pltpu.VMEM((SUBLANES + tm, tn), jnp.float32)],
        compiler_params=_params(2),
        name="short_conv_in",
    )(a, w_in, w_in, w_in, conv_w)


def _ffn_up_body(tiles_per_seq, a_ref, wg_ref, wv_ref, cwg_ref, cwv_ref, o_ref,
                 carry_g_ref, carry_v_ref, stage_g_ref, stage_v_ref):
    i, j = pl.program_id(0), pl.program_id(1)
    a = a_ref[...]
    first = i % tiles_per_seq == 0
    taps = cwg_ref.shape[0]
    tm = a.shape[0]
    gate = _causal_conv(_dot(a, wg_ref[...]), cwg_ref, carry_g_ref.at[j], stage_g_ref, first,
                        taps=taps, halo=SUBLANES, row_chunk=tm)
    val = _causal_conv(_dot(a, wv_ref[...]), cwv_ref, carry_v_ref.at[j], stage_v_ref, first,
                       taps=taps, halo=SUBLANES, row_chunk=tm)
    o_ref[...] = (_silu(gate) * val).astype(o_ref.dtype)


def _ffn_up(a, w_up, conv_w, seq, tm, tn):
    m, k = a.shape
    f = w_up.shape[1] // 2
    nj = f // tn
    taps = conv_w.shape[0]
    wspec = lambda part: pl.BlockSpec((k, tn), lambda i, j: (0, part * nj + j))
    cspec = lambda part: pl.BlockSpec((taps, tn), lambda i, j: (0, part * nj + j))
    return pl.pallas_call(
        functools.partial(_ffn_up_body, seq // tm),
        grid=(m // tm, nj),
        in_specs=[pl.BlockSpec((tm, k), lambda i, j: (i, 0)),
                  wspec(0), wspec(1), cspec(0), cspec(1)],
        out_specs=pl.BlockSpec((tm, tn), lambda i, j: (i, j)),
        out_shape=jax.ShapeDtypeStruct((m, f), jnp.bfloat16),
        scratch_shapes=[pltpu.VMEM((nj, SUBLANES, tn), jnp.float32),
                        pltpu.VMEM((nj, SUBLANES, tn), jnp.float32),
                        pltpu.VMEM((SUBLANES + tm, tn), jnp.float32),
                        pltpu.VMEM((SUBLANES + tm, tn), jnp.float32)],
        compiler_params=_params(2),
        name="ffn_up",
    )(a, w_up, w_up, conv_w, conv_w)


CONFORMER_HALO = 32
CONFORMER_ROW_CHUNK = 64


def _conformer_in_body(tiles_per_seq, a_ref, wv_ref, wg_ref, bv_ref, bg_ref, cw_ref, cb_ref,
                       o_ref, carry_ref, stage_ref):
    i, j = pl.program_id(0), pl.program_id(1)
    a = a_ref[...]
    val = _dot(a, wv_ref[...]) + bv_ref[...]
    gate = _dot(a, wg_ref[...]) + bg_ref[...]
    u = val * jax.nn.sigmoid(gate)
    row_chunk = min(CONFORMER_ROW_CHUNK, u.shape[0])
    conv = _causal_conv(u, cw_ref, carry_ref.at[j], stage_ref, i % tiles_per_seq == 0,
                        taps=cw_ref.shape[0], halo=CONFORMER_HALO, row_chunk=row_chunk)
    o_ref[...] = conv + cb_ref[...]


def _conformer_in(a, w_pw1, b_pw1, dw_w, dw_b, seq, tm, tn):
    m, k = a.shape
    d = w_pw1.shape[1] // 2
    nj = d // tn
    taps = dw_w.shape[0]
    assert taps - 1 <= CONFORMER_HALO <= tm
    wspec = lambda part: pl.BlockSpec((k, tn), lambda i, j: (0, part * nj + j))
    bspec = lambda part: pl.BlockSpec((1, tn), lambda i, j: (0, part * nj + j))
    b2 = b_pw1.reshape(1, 2 * d)
    return pl.pallas_call(
        functools.partial(_conformer_in_body, seq // tm),
        grid=(m // tm, nj),
        in_specs=[pl.BlockSpec((tm, k), lambda i, j: (i, 0)),
                  wspec(0), wspec(1), bspec(0), bspec(1),
                  pl.BlockSpec((taps, tn), lambda i, j: (0, j)),
                  pl.BlockSpec((1, tn), lambda i, j: (0, j))],
        out_specs=pl.BlockSpec((tm, tn), lambda i, j: (i, j)),
        out_shape=jax.ShapeDtypeStruct((m, d), jnp.float32),
        scratch_shapes=[pltpu.VMEM((nj, CONFORMER_HALO, tn), jnp.float32),
                        pltpu.VMEM((CONFORMER_HALO + tm, tn), jnp.float32)],
        compiler_params=_params(2),
        name="conformer_in",
    )(a, w_pw1, w_pw1, b2, b2, dw_w, dw_b.reshape(1, d))


def kernel(x, g_mix_pre, g_mix_post, g_ffn_pre, g_ffn_post, a_w_in, a_conv_w, a_w_out, b_w_pw1, b_b_pw1, b_dw_w, b_dw_b, b_ln_g, b_ln_b, b_w_pw2, b_b_pw2, f_w_up, f_conv_w, f_w_down):
    batch, seq, d = x.shape
    depth = g_mix_pre.shape[0]
    m = batch * seq
    bf16 = jnp.bfloat16

    tm = _pick(seq, (1024, 512, 256, 128, 64, 32))
    tm_down = _pick(seq, (512, 256, 128, 64, 32))
    tm_row = _pick(seq, (256, 128, 64, 32))
    tn_d = _pick(d, (512, 256, 128))
    tn_f = _pick(f_w_up.shape[2] // 2, (512, 256, 128))

    xf = x.reshape(m, d)
    normed = _rmsnorm(xf, g_mix_pre[0], tm_row)
    for i in range(depth):
        j = i // 2
        if i % 2 == 0:
            z = _short_conv_in(normed, a_w_in[j].astype(bf16), a_conv_w[j], seq, tm, tn_d)
            y = _matmul(z, a_w_out[j].astype(bf16), None, tm, tn_d)
        else:
            h = _conformer_in(normed, b_w_pw1[j].astype(bf16), b_b_pw1[j], b_dw_w[j], b_dw_b[j],
                              seq, tm, tn_d)
            h = _ln_silu(h, b_ln_g[j], b_ln_b[j], tm_row)
            y = _matmul(h, b_w_pw2[j].astype(bf16), b_b_pw2[j], tm, tn_d)
        xf, normed = _residual_norm(xf, y, g_mix_post[i], g_ffn_pre[i], tm_row)
        act = _ffn_up(normed, f_w_up[i].astype(bf16), f_conv_w[i], seq, tm, tn_f)
        y = _matmul(act, f_w_down[i].astype(bf16), None, tm_down, tn_d)
        g_next = g_mix_pre[i + 1] if i + 1 < depth else None
        xf, normed = _residual_norm(xf, y, g_ffn_post[i], g_next, tm_row)
    return xf.reshape(batch, seq, d)
```

```python
import functools

import jax
import jax.numpy as jnp
from jax import lax
from jax.experimental import pallas as pl
from jax.experimental.pallas import tpu as pltpu

RMS_EPS = 1e-6
LN_EPS = 1e-5

V7X_VMEM_BYTES = 64 * 1024 * 1024
VMEM_LIMIT_BYTES = V7X_VMEM_BYTES - 8 * 1024 * 1024
SUBLANES = 8
LANES = 128


def _pick(dim, candidates):
    for c in candidates:
        if c <= dim and dim % c == 0:
            return c
    return dim


def _params(n_axes):
    return pltpu.CompilerParams(
        dimension_semantics=("arbitrary",) * n_axes,
        vmem_limit_bytes=VMEM_LIMIT_BYTES,
    )


def _rms_scale(v):
    return lax.rsqrt(jnp.mean(v * v, axis=-1, keepdims=True) + RMS_EPS)


def _silu(v):
    return v * jax.nn.sigmoid(v)


def _rmsnorm_body(x_ref, g_ref, o_ref):
    x = x_ref[...]
    o_ref[...] = (x * _rms_scale(x) * g_ref[...]).astype(o_ref.dtype)


def _rmsnorm(x, g, tm):
    m, d = x.shape
    return pl.pallas_call(
        _rmsnorm_body,
        grid=(m // tm,),
        in_specs=[pl.BlockSpec((tm, d), lambda i: (i, 0)),
                  pl.BlockSpec((1, d), lambda i: (0, 0))],
        out_specs=pl.BlockSpec((tm, d), lambda i: (i, 0)),
        out_shape=jax.ShapeDtypeStruct((m, d), jnp.bfloat16),
        compiler_params=_params(1),
        name="rmsnorm",
    )(x, g.reshape(1, d))


def _residual_norm_body(x_ref, y_ref, gpost_ref, gnext_ref, xo_ref, no_ref):
    y = y_ref[...]
    xn = x_ref[...] + y * _rms_scale(y) * gpost_ref[...]
    xo_ref[...] = xn
    no_ref[...] = (xn * _rms_scale(xn) * gnext_ref[...]).astype(no_ref.dtype)


def _residual_body(x_ref, y_ref, gpost_ref, xo_ref):
    y = y_ref[...]
    xo_ref[...] = x_ref[...] + y * _rms_scale(y) * gpost_ref[...]


def _residual_norm(x, y, g_post, g_next, tm):
    m, d = x.shape
    row = pl.BlockSpec((tm, d), lambda i: (i, 0))
    vec = pl.BlockSpec((1, d), lambda i: (0, 0))
    if g_next is None:
        return pl.pallas_call(
            _residual_body,
            grid=(m // tm,),
            in_specs=[row, row, vec],
            out_specs=row,
            out_shape=jax.ShapeDtypeStruct((m, d), jnp.float32),
            compiler_params=_params(1),
            name="residual",
        )(x, y, g_post.reshape(1, d)), None
    return pl.pallas_call(
        _residual_norm_body,
        grid=(m // tm,),
        in_specs=[row, row, vec, vec],
        out_specs=[row, row],
        out_shape=[jax.ShapeDtypeStruct((m, d), jnp.float32),
                   jax.ShapeDtypeStruct((m, d), jnp.bfloat16)],
        compiler_params=_params(1),
        name="residual_norm",
    )(x, y, g_post.reshape(1, d), g_next.reshape(1, d))


def _ln_silu_body(h_ref, g_ref, b_ref, o_ref):
    h = h_ref[...]
    mu = jnp.mean(h, axis=-1, keepdims=True)
    c = h - mu
    var = jnp.mean(c * c, axis=-1, keepdims=True)
    y = c * lax.rsqrt(var + LN_EPS) * g_ref[...] + b_ref[...]
    o_ref[...] = _silu(y).astype(o_ref.dtype)


def _ln_silu(h, g, b, tm):
    m, d = h.shape
    row = pl.BlockSpec((tm, d), lambda i: (i, 0))
    vec = pl.BlockSpec((1, d), lambda i: (0, 0))
    return pl.pallas_call(
        _ln_silu_body,
        grid=(m // tm,),
        in_specs=[row, vec, vec],
        out_specs=row,
        out_shape=jax.ShapeDtypeStruct((m, d), jnp.bfloat16),
        compiler_params=_params(1),
        name="ln_silu",
    )(h, g.reshape(1, d), b.reshape(1, d))


def _dot(a, b):
    return jnp.dot(a, b, preferred_element_type=jnp.float32)


def _matmul_body(a_ref, w_ref, o_ref):
    o_ref[...] = _dot(a_ref[...], w_ref[...])


def _matmul_bias_body(a_ref, w_ref, b_ref, o_ref):
    o_ref[...] = _dot(a_ref[...], w_ref[...]) + b_ref[...]


def _matmul(a, w, bias, tm, tn):
    m, k = a.shape
    n = w.shape[1]
    in_specs = [pl.BlockSpec((tm, k), lambda i, j: (i, 0)),
                pl.BlockSpec((k, tn), lambda i, j: (0, j))]
    args = [a, w]
    body = _matmul_body
    if bias is not None:
        in_specs.append(pl.BlockSpec((1, tn), lambda i, j: (0, j)))
        args.append(bias.reshape(1, n))
        body = _matmul_bias_body
    return pl.pallas_call(
        body,
        grid=(m // tm, n // tn),
        in_specs=in_specs,
        out_specs=pl.BlockSpec((tm, tn), lambda i, j: (i, j)),
        out_shape=jax.ShapeDtypeStruct((m, n), jnp.float32),
        compiler_params=_params(2),
        name="matmul",
    )(*args)


DOT_ROWS = 256
ROW_CHUNK = 128
CONV3_HALO = SUBLANES
CONFORMER_HALO = 32
CONFORMER_ROW_CHUNK = 64


def _interleave(w, parts, tn):
    k, pn = w.shape
    n = pn // parts
    w = w.astype(jnp.bfloat16).reshape(k, parts, n // tn, tn)
    return w.transpose(0, 2, 1, 3).reshape(k, pn)


def _up_body(epilogue, nj, tiles_per_seq, halo, n_res, a_ref, w_ref, *rest):
    n_small = len(rest) - 2 - n_res
    small_refs, o_ref = rest[:n_small], rest[n_small]
    res_refs, carry_ref = rest[n_small + 1:-1], rest[-1]
    s = pl.program_id(0)
    tm = a_ref.shape[0]

    @pl.when(s == 0)
    def _():
        for res_ref in res_refs:
            res_ref[...] = jnp.zeros(res_ref.shape, res_ref.dtype)
        carry_ref[...] = jnp.zeros(carry_ref.shape, carry_ref.dtype)

    e = jnp.maximum(s - 1, 0)
    first_in_seq = (e // nj) % tiles_per_seq == 0
    je = e % nj

    def step(read_ref, write_ref):
        read_ref[0:halo, :] = carry_ref[je]
        carry_ref[je] = read_ref[tm:tm + halo, :]
        piece = min(DOT_ROWS, tm) if read_ref is not write_ref else tm
        for r0 in range(0, tm, piece):
            epilogue(first_in_seq, read_ref, *small_refs, o_ref, r0, r0 + piece)
            write_ref[halo + r0:halo + r0 + piece, :] = _dot(a_ref[r0:r0 + piece, :], w_ref[...])

    if n_res == 1:
        step(res_refs[0], res_refs[0])
    else:
        pl.when(s % 2 == 0)(lambda: step(res_refs[1], res_refs[0]))
        pl.when(s % 2 == 1)(lambda: step(res_refs[0], res_refs[1]))


def _up_call(epilogue, a, w, smalls, parts, tn, halo, seq, tm, out_dtype, name, n_res=2):
    m, k = a.shape
    n = w.shape[1] // parts
    nj = n // tn
    total = (m // tm) * nj
    cur = lambda s: jnp.minimum(s, total - 1)
    prev = lambda s: jnp.maximum(s - 1, 0)
    in_specs = [pl.BlockSpec((tm, k), lambda s: (cur(s) // nj, 0), pipeline_mode=pl.Buffered(1)),
                pl.BlockSpec((k, parts * tn), lambda s: (0, cur(s) % nj))]
    for arr, part in smalls:
        in_specs.append(pl.BlockSpec((arr.shape[0], tn),
                                     lambda s, part=part: (0, part * nj + prev(s) % nj)))
    return pl.pallas_call(
        functools.partial(_up_body, epilogue, nj, seq // tm, halo, n_res),
        grid=(total + 1,),
        in_specs=in_specs,
        out_specs=pl.BlockSpec((tm, tn), lambda s: (prev(s) // nj, prev(s) % nj)),
        out_shape=jax.ShapeDtypeStruct((m, n), out_dtype),
        scratch_shapes=[pltpu.VMEM((halo + tm, parts * tn), jnp.float32)] * n_res
        + [pltpu.VMEM((nj, halo, parts * tn), jnp.float32)],
        compiler_params=_params(1),
        name=name,
    )(a, w, *[arr for arr, _ in smalls])


def _load_with_halo(res_ref, r0, rows, halo, cols, first_in_seq):
    v = res_ref[r0:r0 + halo + rows, cols]
    if r0 == 0:
        v = jnp.concatenate([jnp.where(first_in_seq, 0.0, v[:halo]), v[halo:]], axis=0)
    return v


def _conv3(v, w):
    s1 = pltpu.roll(v, 1, 0)[SUBLANES:]
    s2 = pltpu.roll(v, 2, 0)[SUBLANES:]
    return w[0] * s2 + w[1] * s1 + w[2] * v[SUBLANES:]


def _tap_weights(cw_ref, cols, rows):
    return [jnp.broadcast_to(cw_ref[k:k + 1, cols], (rows, LANES)) for k in range(cw_ref.shape[0])]


def _short_conv_epilogue(first_in_seq, res_ref, cw_ref, o_ref, row_lo, row_hi):
    tn = o_ref.shape[1]
    rows = min(ROW_CHUNK, row_hi - row_lo)
    for c0 in range(0, tn, LANES):
        cols = [slice(p * tn + c0, p * tn + c0 + LANES) for p in range(3)]
        w = _tap_weights(cw_ref, slice(c0, c0 + LANES), rows)
        for r0 in range(row_lo, row_hi, rows):
            gate_c = _load_with_halo(res_ref, r0, rows, CONV3_HALO, cols[1], first_in_seq)
            h = _load_with_halo(res_ref, r0, rows, CONV3_HALO, cols[2], first_in_seq)
            gate_b = res_ref[CONV3_HALO + r0:CONV3_HALO + r0 + rows, cols[0]]
            o_ref[r0:r0 + rows, c0:c0 + LANES] = (gate_b * _conv3(gate_c * h, w)).astype(o_ref.dtype)


def _short_conv_in(a, w_in, conv_w, seq, tm, tn):
    assert conv_w.shape[0] == 3
    return _up_call(_short_conv_epilogue, a, _interleave(w_in, 3, tn), [(conv_w, 0)], 3, tn,
                    CONV3_HALO, seq, tm, jnp.bfloat16, "short_conv_in", n_res=1)


def _ffn_epilogue(first_in_seq, res_ref, cwg_ref, cwv_ref, o_ref, row_lo, row_hi):
    tn = o_ref.shape[1]
    rows = min(ROW_CHUNK, row_hi - row_lo)
    for c0 in range(0, tn, LANES):
        wg = _tap_weights(cwg_ref, slice(c0, c0 + LANES), rows)
        wv = _tap_weights(cwv_ref, slice(c0, c0 + LANES), rows)
        for r0 in range(row_lo, row_hi, rows):
            gate = _conv3(_load_with_halo(res_ref, r0, rows, CONV3_HALO,
                                          slice(c0, c0 + LANES), first_in_seq), wg)
            val = _conv3(_load_with_halo(res_ref, r0, rows, CONV3_HALO,
                                         slice(tn + c0, tn + c0 + LANES), first_in_seq), wv)
            o_ref[r0:r0 + rows, c0:c0 + LANES] = (_silu(gate) * val).astype(o_ref.dtype)


def _ffn_up(a, w_up, conv_w, seq, tm, tn):
    assert conv_w.shape[0] == 3
    return _up_call(_ffn_epilogue, a, _interleave(w_up, 2, tn), [(conv_w, 0), (conv_w, 1)], 2, tn,
                    CONV3_HALO, seq, tm, jnp.bfloat16, "ffn_up")


def _conformer_epilogue(first_in_seq, res_ref, bv_ref, bg_ref, cw_ref, cb_ref, o_ref,
                        row_lo, row_hi):
    tn = o_ref.shape[1]
    halo, taps = CONFORMER_HALO, cw_ref.shape[0]
    rows = min(CONFORMER_ROW_CHUNK, row_hi - row_lo)
    n = halo + rows
    for c0 in range(0, tn, LANES):
        cols = slice(c0, c0 + LANES)
        gcols = slice(tn + c0, tn + c0 + LANES)
        for r0 in range(row_lo, row_hi, rows):
            val = res_ref[r0:r0 + n, cols] + bv_ref[:, cols]
            gate = res_ref[r0:r0 + n, gcols] + bg_ref[:, cols]
            u = val * jax.nn.sigmoid(gate)
            if r0 == 0:
                u = jnp.concatenate([jnp.where(first_in_seq, 0.0, u[:halo]), u[halo:]], axis=0)
            acc = None
            for r in range(SUBLANES):
                shifted = u if r == 0 else pltpu.roll(u, n - r, 0)
                for q in range(halo // SUBLANES + 1):
                    k = SUBLANES * q + r - (halo - (taps - 1))
                    if 0 <= k < taps:
                        term = shifted[SUBLANES * q:SUBLANES * q + rows] * cw_ref[k:k + 1, cols]
                        acc = term if acc is None else acc + term
            o_ref[r0:r0 + rows, cols] = acc + cb_ref[:, cols]


def _conformer_in(a, w_pw1, b_pw1, dw_w, dw_b, seq, tm, tn):
    d = w_pw1.shape[1] // 2
    assert dw_w.shape[0] - 1 <= CONFORMER_HALO <= tm
    b2 = b_pw1.reshape(1, 2 * d)
    return _up_call(_conformer_epilogue, a, _interleave(w_pw1, 2, tn),
                    [(b2, 0), (b2, 1), (dw_w, 0), (dw_b.reshape(1, d), 0)], 2, tn,
                    CONFORMER_HALO, seq, tm, jnp.float32, "conformer_in")


def kernel(x, g_mix_pre, g_mix_post, g_ffn_pre, g_ffn_post, a_w_in, a_conv_w, a_w_out, b_w_pw1, b_b_pw1, b_dw_w, b_dw_b, b_ln_g, b_ln_b, b_w_pw2, b_b_pw2, f_w_up, f_conv_w, f_w_down):
    batch, seq, d = x.shape
    depth = g_mix_pre.shape[0]
    m = batch * seq
    bf16 = jnp.bfloat16

    tm = _pick(seq, (1024, 512, 256, 128, 64, 32))
    tm_down = _pick(seq, (512, 256, 128, 64, 32))
    tm_row = _pick(seq, (256, 128, 64, 32))
    tn_d = _pick(d, (512, 256, 128))
    tn_f = _pick(f_w_up.shape[2] // 2, (512, 256, 128))

    xf = x.reshape(m, d)
    normed = _rmsnorm(xf, g_mix_pre[0], tm_row)
    for i in range(depth):
        j = i // 2
        if i % 2 == 0:
            z = _short_conv_in(normed, a_w_in[j], a_conv_w[j], seq, tm, tn_d)
            y = _matmul(z, a_w_out[j].astype(bf16), None, tm, tn_d)
        else:
            h = _conformer_in(normed, b_w_pw1[j], b_b_pw1[j], b_dw_w[j], b_dw_b[j],
                              seq, tm, tn_d)
            h = _ln_silu(h, b_ln_g[j], b_ln_b[j], tm_row)
            y = _matmul(h, b_w_pw2[j].astype(bf16), b_b_pw2[j], tm, tn_d)
        xf, normed = _residual_norm(xf, y, g_mix_post[i], g_ffn_pre[i], tm_row)
        act = _ffn_up(normed, f_w_up[i], f_conv_w[i], seq, tm, tn_f)
        y = _matmul(act, f_w_down[i].astype(bf16), None, tm_down, tn_d)
        g_next = g_mix_pre[i + 1] if i + 1 < depth else None
        xf, normed = _residual_norm(xf, y, g_ffn_post[i], g_next, tm_row)
    return xf.reshape(batch, seq, d)
```
